```python
import math
import jax
import jax.numpy as jnp
from jax import lax
import numpy as np

D_MODEL = 4096
BATCH = 32
SEQ = 256
DEPTH = 4
DEC_BATCH = 4
DEC_SEQ = 2048
PAST_LEN = 256

GRID_W = 64
N_MIXERS = 2
N_ATTN_LAYERS = (DEPTH + 1) // 2
N_SSM_LAYERS = DEPTH // 2
NORM_EPS = 1e-6
HEAD_DIM = 128
N_HEADS = D_MODEL // (2 * HEAD_DIM)
ROPE_THETA = 10000.0
Q_BLOCK = 128
SUBLN_EPS = 1e-5
EXPAND = 2
D_INNER = EXPAND * D_MODEL
SSM_HEAD_DIM = 64
SSM_HEADS = D_INNER // SSM_HEAD_DIM
N_GROUPS = 8
D_STATE = 128
D_CONV = 5
CHUNK = 64
SSM_CONV_DIM = D_INNER + 2 * N_GROUPS * D_STATE
SSM_IN_DIM = D_INNER + SSM_CONV_DIM + 2 * SSM_HEADS
DT_MIN = 0.001
DT_MAX = 0.1

kernel_name = "hybrid_diffattn_mamba2_prefix_dit_step"


def rms_norm(x, w, eps=NORM_EPS):
    xf = x.astype(jnp.float32)
    y = xf * lax.rsqrt(jnp.mean(xf * xf, axis=-1, keepdims=True) + eps)
    return (y * w.astype(jnp.float32)).astype(x.dtype)


def adaln(cvec, w_ada_l, b_ada_l):
    m = jax.nn.silu(cvec) @ w_ada_l + b_ada_l
    return jnp.split(m, 3, axis=-1)


def axial_rope_tables(n_tokens):
    n_rows = n_tokens // GRID_W
    rows = jnp.repeat(jnp.arange(n_rows, dtype=jnp.float32), GRID_W)
    cols = jnp.tile(jnp.arange(GRID_W, dtype=jnp.float32), n_rows)
    axis_dim = HEAD_DIM // 2
    inv_freq = ROPE_THETA ** (-jnp.arange(0, axis_dim, 2, dtype=jnp.float32) / axis_dim)
    ang = jnp.stack([rows[:, None] * inv_freq, cols[:, None] * inv_freq], axis=1)
    return jnp.cos(ang), jnp.sin(ang)


def apply_axial_rope(x, cos, sin):
    b, l = x.shape[:2]
    xr = x.astype(jnp.float32).reshape(b, l, N_HEADS, 2, 2, 2, HEAD_DIM // 4)
    x1, x2 = xr[..., 0, :], xr[..., 1, :]
    c_ = cos[None, :, None, None]
    s_ = sin[None, :, None, None]
    out = jnp.stack([x1 * c_ - x2 * s_, x2 * c_ + x1 * s_], axis=-2)
    return out.reshape(x.shape).astype(x.dtype)


def diff_lambda(lq1, lk1, lq2, lk2, lambda_init):
    f32 = jnp.float32
    return (jnp.exp(jnp.sum(lq1.astype(f32) * lk1.astype(f32)))
            - jnp.exp(jnp.sum(lq2.astype(f32) * lk2.astype(f32))) + lambda_init)


def attn_project(h, w_in):
    b, l, _ = h.shape
    q, k, v, g = jnp.split(h @ w_in, 4, axis=-1)
    return (q.reshape(b, l, N_HEADS, 2, HEAD_DIM), k.reshape(b, l, N_HEADS, 2, HEAD_DIM),
            v.reshape(b, l, N_HEADS, 2 * HEAD_DIM), g)


def diff_softmax_attention(q, k, v, lam):
    b, lq = q.shape[:2]
    nb = lq // Q_BLOCK
    qb = jnp.moveaxis(q.reshape(b, nb, Q_BLOCK, N_HEADS, 2, HEAD_DIM), 1, 0)
    scale = HEAD_DIM ** -0.5

    def one_block(qblk):
        s = jnp.einsum('bqhtd,bkhtd->bhtqk', qblk, k).astype(jnp.float32) * scale
        p = jax.nn.softmax(s, axis=-1)
        w = (p[:, :, 0] - lam * p[:, :, 1]).astype(v.dtype)
        return jnp.einsum('bhqk,bkhe->bqhe', w, v)

    o = lax.map(one_block, qb)
    return jnp.moveaxis(o, 0, 1).reshape(b, lq, N_HEADS, 2 * HEAD_DIM)


def attn_output(o, g, subln_w, lambda_init, w_out):
    b, l = o.shape[:2]
    o = rms_norm(o, subln_w, SUBLN_EPS) * (1.0 - lambda_init)
    return (o.reshape(b, l, D_MODEL) * jax.nn.silu(g)) @ w_out


def diff_attn_context(h, w_in, w_out, lam, lambda_init, subln_w):
    q, k, v, g = attn_project(h, w_in)
    o = diff_softmax_attention(q, k, v, lam)
    return attn_output(o, g, subln_w, lambda_init, w_out), k, v


def diff_attn_latent(h, k_ctx, v_ctx, cos, sin, w_in, w_out, lam, lambda_init, subln_w):
    q, k, v, g = attn_project(h, w_in)
    q = apply_axial_rope(q, cos, sin)
    k = apply_axial_rope(k, cos, sin)
    k_all = jnp.concatenate([k, k_ctx.astype(k.dtype)], axis=1)
    v_all = jnp.concatenate([v, v_ctx.astype(v.dtype)], axis=1)
    o = diff_softmax_attention(q, k_all, v_all, lam)
    return attn_output(o, g, subln_w, lambda_init, w_out)


def centered_depthwise_conv(x, w, bias):
    out = lax.conv_general_dilated(
        x, w[:, None, :].astype(x.dtype), window_strides=(1,),
        padding=[(D_CONV // 2, D_CONV // 2)],
        dimension_numbers=('NWC', 'WIO', 'NWC'), feature_group_count=x.shape[-1])
    return out + bias


def ssd_scan(x, dt, a_coef, bm, cm, h0):
    f32 = jnp.float32
    b, l, nh, hp = x.shape
    ng, ds = bm.shape[2], bm.shape[3]
    hpg = nh // ng
    nc = l // CHUNK
    xdt = (x.astype(f32) * dt[..., None]).reshape(b, nc, CHUNK, ng, hpg, hp)
    bc = bm.astype(f32).reshape(b, nc, CHUNK, ng, ds)
    cc = cm.astype(f32).reshape(b, nc, CHUNK, ng, ds)
    a = jnp.moveaxis((dt * a_coef).reshape(b, nc, CHUNK, ng, hpg), 2, -1)
    a_cs = jnp.cumsum(a, axis=-1)
    causal = jnp.tril(jnp.ones((CHUNK, CHUNK), dtype=bool))
    seg = a_cs[..., :, None] - a_cs[..., None, :]
    decay = jnp.where(causal, jnp.exp(jnp.where(causal, seg, 0.0)), 0.0)
    cb = jnp.einsum('bcign,bcjgn->bcgij', cc, bc)
    y_diag = jnp.einsum('bcgij,bcgkij,bcjgkp->bcigkp', cb, decay, xdt)
    decay_to_end = jnp.exp(a_cs[..., -1:] - a_cs)
    states = jnp.einsum('bcjgn,bcgkj,bcjgkp->bcgkpn', bc, decay_to_end, xdt)
    chunk_decay = jnp.exp(a_cs[..., -1])

    def step(hc, inp):
        st, dec = inp
        return hc * dec[..., None, None] + st, hc

    h0g = h0.astype(f32).reshape(b, ng, hpg, hp, ds)
    h_final, h_prev = lax.scan(step, h0g, (jnp.moveaxis(states, 1, 0), jnp.moveaxis(chunk_decay, 1, 0)))
    h_prev = jnp.moveaxis(h_prev, 0, 1)
    y_off = jnp.einsum('bcign,bcgkpn,bcgki->bcigkp', cc, h_prev, jnp.exp(a_cs))
    y = (y_diag + y_off).reshape(b, l, nh, hp).astype(x.dtype)
    return y, h_final.reshape(b, nh, hp, ds)


def ssm_mixer(h, h0_fwd, h0_bwd, w_in, conv_w, conv_b, dt_bias, a_log, d_skip, norm_w, w_out):
    b, l, _ = h.shape
    zxbcdt = h @ w_in
    z = zxbcdt[..., :D_INNER]
    xbc = zxbcdt[..., D_INNER:D_INNER + SSM_CONV_DIM]
    dt_raw = zxbcdt[..., D_INNER + SSM_CONV_DIM:]
    xbc = jax.nn.silu(centered_depthwise_conv(xbc, conv_w, conv_b))
    xs = xbc[..., :D_INNER].reshape(b, l, SSM_HEADS, SSM_HEAD_DIM)
    bm = xbc[..., D_INNER:D_INNER + N_GROUPS * D_STATE].reshape(b, l, N_GROUPS, D_STATE)
    cm = xbc[..., D_INNER + N_GROUPS * D_STATE:].reshape(b, l, N_GROUPS, D_STATE)
    dt = jax.nn.softplus(dt_raw.astype(jnp.float32).reshape(b, l, 2, SSM_HEADS)
                         + dt_bias.astype(jnp.float32))
    a_coef = -jnp.exp(a_log.astype(jnp.float32))
    y_f, hf = ssd_scan(xs, dt[:, :, 0], a_coef[0], bm, cm, h0_fwd)
    y_b, hb = ssd_scan(jnp.flip(xs, 1), jnp.flip(dt[:, :, 1], 1), a_coef[1],
                       jnp.flip(bm, 1), jnp.flip(cm, 1), h0_bwd)
    y = y_f + jnp.flip(y_b, 1) + xs * d_skip[:, None]
    y = rms_norm(y.reshape(b, l, D_INNER) * jax.nn.silu(z), norm_w)
    return y @ w_out, hf, hb


def setup_inputs(seed: int = 0) -> dict:
    key = jax.random.key(seed)
    ks = jax.random.split(key, 32)
    f32 = jnp.float32

    def nrm(k, shape, scale):
        return jax.random.normal(k, shape, f32) * scale

    dt0 = jnp.exp(jax.random.uniform(ks[20], (N_SSM_LAYERS, 2, SSM_HEADS), f32,
                                     math.log(DT_MIN), math.log(DT_MAX)))
    return {
        'x_prompt': nrm(ks[0], (BATCH, SEQ, D_MODEL), 1.0),
        'x_sample': nrm(ks[1], (DEC_BATCH, DEC_SEQ, D_MODEL), 1.0),
        'cache_attn_k': nrm(ks[2], (DEC_BATCH, N_ATTN_LAYERS, PAST_LEN, N_HEADS, 2, HEAD_DIM), 1.0),
        'cache_attn_v': nrm(ks[3], (DEC_BATCH, N_ATTN_LAYERS, PAST_LEN, N_HEADS, 2 * HEAD_DIM), 1.0),
        'state_ssm': nrm(ks[4], (DEC_BATCH, N_SSM_LAYERS, 2, SSM_HEADS, SSM_HEAD_DIM, D_STATE), 0.1),
        'c': nrm(ks[5], (DEC_BATCH, D_MODEL), 1.0),
        'c_ctx': nrm(ks[6], (D_MODEL,), 1.0),
        'norm_w': 1.0 + nrm(ks[7], (DEPTH, D_MODEL), 0.02),
        'w_ada': nrm(ks[8], (DEPTH, D_MODEL, 3 * D_MODEL), 0.5 * D_MODEL ** -0.5),
        'b_ada': nrm(ks[9], (DEPTH, 3 * D_MODEL), 0.02),
        'w_attn_in': nrm(ks[10], (N_ATTN_LAYERS, D_MODEL, 4 * D_MODEL), D_MODEL ** -0.5),
        'attn_lambda_q1': nrm(ks[11], (N_ATTN_LAYERS, HEAD_DIM), 0.1),
        'attn_lambda_k1': nrm(ks[12], (N_ATTN_LAYERS, HEAD_DIM), 0.1),
        'attn_lambda_q2': nrm(ks[13], (N_ATTN_LAYERS, HEAD_DIM), 0.1),
        'attn_lambda_k2': nrm(ks[14], (N_ATTN_LAYERS, HEAD_DIM), 0.1),
        'attn_subln_w': 1.0 + nrm(ks[15], (N_ATTN_LAYERS, 2 * HEAD_DIM), 0.02),
        'w_attn_out': nrm(ks[16], (N_ATTN_LAYERS, D_MODEL, D_MODEL), D_MODEL ** -0.5),
        'w_ssm_in': nrm(ks[17], (N_SSM_LAYERS, D_MODEL, SSM_IN_DIM), D_MODEL ** -0.5),
        'ssm_conv_w': nrm(ks[18], (N_SSM_LAYERS, D_CONV, SSM_CONV_DIM), D_CONV ** -0.5),
        'ssm_conv_b': nrm(ks[19], (N_SSM_LAYERS, SSM_CONV_DIM), 0.02),
        'ssm_dt_bias': dt0 + jnp.log(-jnp.expm1(-dt0)),
        'ssm_A_log': jnp.log(jax.random.uniform(ks[21], (N_SSM_LAYERS, 2, SSM_HEADS), f32, 1.0, 16.0)),
        'ssm_D': 1.0 + nrm(ks[22], (N_SSM_LAYERS, SSM_HEADS), 0.1),
        'ssm_norm_w': 1.0 + nrm(ks[23], (N_SSM_LAYERS, D_INNER), 0.02),
        'w_ssm_out': nrm(ks[24], (N_SSM_LAYERS, D_INNER, D_MODEL), D_INNER ** -0.5),
        'final_norm_w': 1.0 + nrm(ks[25], (D_MODEL,), 0.02),
    }


def reference(x_prompt, x_sample, cache_attn_k, cache_attn_v, state_ssm, c, c_ctx,
              norm_w, w_ada, b_ada, w_attn_in, attn_lambda_q1, attn_lambda_k1,
              attn_lambda_q2, attn_lambda_k2, attn_subln_w, w_attn_out,
              w_ssm_in, ssm_conv_w, ssm_conv_b, ssm_dt_bias, ssm_A_log, ssm_D,
              ssm_norm_w, w_ssm_out, final_norm_w):
    n_lat = x_sample.shape[1]
    cos, sin = axial_rope_tables(n_lat)
    b_ctx = x_prompt.shape[0]
    xp, xs = x_prompt, x_sample
    new_k, new_v, new_h = [], [], []
    for layer in range(DEPTH):
        sh_p, sc_p, g_p = adaln(c_ctx, w_ada[layer], b_ada[layer])
        sh_s, sc_s, g_s = [m[:, None] for m in adaln(c, w_ada[layer], b_ada[layer])]
        hp = rms_norm(xp, norm_w[layer]) * (1.0 + sc_p) + sh_p
        hs = rms_norm(xs, norm_w[layer]) * (1.0 + sc_s) + sh_s
        idx = layer // N_MIXERS
        if layer % N_MIXERS == 0:
            lambda_init = 0.8 - 0.6 * math.exp(-0.3 * layer)
            lam = diff_lambda(attn_lambda_q1[idx], attn_lambda_k1[idx],
                              attn_lambda_q2[idx], attn_lambda_k2[idx], lambda_init)
            out_p, k_p, v_p = diff_attn_context(hp, w_attn_in[idx], w_attn_out[idx], lam,
                                                lambda_init, attn_subln_w[idx])
            out_s = diff_attn_latent(hs, cache_attn_k[:, idx], cache_attn_v[:, idx], cos, sin,
                                     w_attn_in[idx], w_attn_out[idx], lam, lambda_init,
                                     attn_subln_w[idx])
            new_k.append(k_p)
            new_v.append(v_p)
        else:
            zeros = jnp.zeros((b_ctx, SSM_HEADS, SSM_HEAD_DIM, D_STATE), jnp.float32)
            out_p, hf, hb = ssm_mixer(hp, zeros, zeros, w_ssm_in[idx], ssm_conv_w[idx],
                                      ssm_conv_b[idx], ssm_dt_bias[idx], ssm_A_log[idx],
                                      ssm_D[idx], ssm_norm_w[idx], w_ssm_out[idx])
            out_s, _, _ = ssm_mixer(hs, state_ssm[:, idx, 0], state_ssm[:, idx, 1],
                                    w_ssm_in[idx], ssm_conv_w[idx], ssm_conv_b[idx],
                                    ssm_dt_bias[idx], ssm_A_log[idx], ssm_D[idx],
                                    ssm_norm_w[idx], w_ssm_out[idx])
            new_h.append(jnp.stack([hf, hb], axis=1))
        xp = xp + g_p * out_p
        xs = xs + g_s * out_s
    y_prompt = rms_norm(xp, final_norm_w)
    y_sample = rms_norm(xs, final_norm_w)
    new_attn_k = jnp.stack(new_k, axis=1)
    new_attn_v = jnp.stack(new_v, axis=1)
    new_ssm_state = jnp.stack(new_h, axis=1)
    return (y_prompt, y_sample, new_attn_k, new_attn_v, new_ssm_state)
```

```python
import functools
import math

import jax
import jax.numpy as jnp
from jax import lax
from jax.experimental import pallas as pl
from jax.experimental.pallas import tpu as pltpu

F32 = jnp.float32
BF16 = jnp.bfloat16

NORM_EPS = 1e-6
SUBLN_EPS = 1e-5
GRID_W = 64
ROPE_THETA = 10000.0
D_CONV = 5

V7X_VMEM_BYTES = 64 * 2**20
LANES = 128
SUBLANES = 8
MOD_ROWS = 8
SSD_CHUNK = 128


def _cparams(semantics, vmem_bytes):
    limit = min(int(vmem_bytes * 1.25) + (4 << 20), V7X_VMEM_BYTES - (6 << 20))
    return pltpu.CompilerParams(dimension_semantics=semantics, vmem_limit_bytes=limit)


def _tile(n, target, quantum=LANES):
    if n <= target:
        return n
    t = (target // quantum) * quantum
    while t >= quantum:
        if n % t == 0:
            return t
        t -= quantum
    raise ValueError(f"no tile for {n} under {target}")


def _adaln_kernel(c_ref, w_ref, b_ref, o_ref):
    s = jax.nn.silu(c_ref[...]).astype(BF16)
    o_ref[...] = jnp.dot(s, w_ref[...].astype(BF16), preferred_element_type=F32) + b_ref[...]


def _adaln(cvecs, w_ada, b_ada):
    depth, d, n = w_ada.shape
    tn = _tile(n, 512)
    vmem = 2 * d * tn * 4 + d * tn * 2 + 2 * MOD_ROWS * d * 4
    return pl.pallas_call(
        _adaln_kernel,
        grid=(depth, n // tn),
        in_specs=[pl.BlockSpec((MOD_ROWS, d), lambda l, j: (0, 0)),
                  pl.BlockSpec((None, d, tn), lambda l, j: (l, 0, j)),
                  pl.BlockSpec((None, 1, tn), lambda l, j: (l, 0, j))],
        out_specs=pl.BlockSpec((None, MOD_ROWS, tn), lambda l, j: (l, 0, j)),
        out_shape=jax.ShapeDtypeStruct((depth, MOD_ROWS, n), F32),
        compiler_params=_cparams(("parallel", "parallel"), vmem),
        name="adaln",
    )(cvecs, w_ada, b_ada.reshape(depth, 1, n))


def _norm_mod_kernel(x_ref, w_ref, mod_ref, o_ref):
    x = x_ref[...]
    y = x * lax.rsqrt(jnp.mean(x * x, axis=-1, keepdims=True) + NORM_EPS) * w_ref[...]
    o_ref[...] = (y * (1.0 + mod_ref[1:2, :]) + mod_ref[0:1, :]).astype(o_ref.dtype)


def _norm_mod(x, norm_w, mod, layer, row_of_tile, tm):
    m, d = x.shape
    vmem = 2 * tm * d * 4 + 2 * tm * d * 2 + 3 * tm * d * 4
    return pl.pallas_call(
        _norm_mod_kernel,
        grid=(m // tm,),
        in_specs=[pl.BlockSpec((tm, d), lambda i: (i, 0)),
                  pl.BlockSpec((None, 1, d), lambda i: (layer, 0, 0)),
                  pl.BlockSpec((None, None, 3, d), lambda i: (layer, row_of_tile(i), 0, 0))],
        out_specs=pl.BlockSpec((tm, d), lambda i: (i, 0)),
        out_shape=jax.ShapeDtypeStruct((m, d), BF16),
        compiler_params=_cparams(("parallel",), vmem),
        name="norm_mod",
    )(x, norm_w.reshape(norm_w.shape[0], 1, d), mod)


def _final_norm_kernel(x_ref, w_ref, o_ref):
    x = x_ref[...]
    o_ref[...] = x * lax.rsqrt(jnp.mean(x * x, axis=-1, keepdims=True) + NORM_EPS) * w_ref[...]


def _final_norm(x, w, tm):
    m, d = x.shape
    return pl.pallas_call(
        _final_norm_kernel,
        grid=(m // tm,),
        in_specs=[pl.BlockSpec((tm, d), lambda i: (i, 0)), pl.BlockSpec((1, d), lambda i: (0, 0))],
        out_specs=pl.BlockSpec((tm, d), lambda i: (i, 0)),
        out_shape=jax.ShapeDtypeStruct((m, d), F32),
        compiler_params=_cparams(("parallel",), 6 * tm * d * 4),
        name="final_norm",
    )(x, w.reshape(1, d))


def _rope_store(acc, o_ref, cos_ref, s1_ref, s2_ref):
    cos, s1, s2 = cos_ref[...], s1_ref[...], s2_ref[...]
    quarter = LANES // 4
    for g in range(acc.shape[1] // LANES):
        x = acc[:, g * LANES:(g + 1) * LANES]
        y = x * cos + pltpu.roll(x, LANES - quarter, axis=1) * s1 + pltpu.roll(x, quarter, axis=1) * s2
        o_ref[:, g * LANES:(g + 1) * LANES] = y.astype(o_ref.dtype)


def _attn_inproj_kernel(*refs, n_sec, rope, emit_kv):
    h_ref, w_ref = refs[0], refs[1]
    pos = 2
    if rope:
        cos_ref, s1_ref, s2_ref = refs[pos:pos + 3]
        pos += 3
    o_ref = refs[pos]
    kv_ref = refs[pos + 1] if emit_kv else None
    sec = pl.program_id(1) // n_sec
    acc = jnp.dot(h_ref[...], w_ref[...], preferred_element_type=F32)

    @pl.when(sec <= 1)
    def _():
        if rope:
            _rope_store(acc, o_ref, cos_ref, s1_ref, s2_ref)
        else:
            o_ref[...] = acc.astype(o_ref.dtype)

    @pl.when(sec == 2)
    def _():
        o_ref[...] = acc.astype(o_ref.dtype)

    @pl.when(sec == 3)
    def _():
        o_ref[...] = jax.nn.silu(acc).astype(o_ref.dtype)

    if emit_kv:
        @pl.when(jnp.logical_or(sec == 1, sec == 2))
        def _():
            kv_ref[...] = acc


def _attn_inproj(h, w, rope_tabs, emit_kv, tm, tn):
    m, d = h.shape
    n = w.shape[1]
    n_sec = d // tn
    rope = rope_tabs is not None
    in_specs = [pl.BlockSpec((tm, d), lambda i, j: (i, 0)), pl.BlockSpec((d, tn), lambda i, j: (0, j))]
    args = [h, w]
    if rope:
        tiles_per_seq = rope_tabs[0].shape[0] // tm
        for t in rope_tabs:
            in_specs.append(pl.BlockSpec((tm, LANES), lambda i, j: (i % tiles_per_seq, 0)))
            args.append(t)
    out_specs = [pl.BlockSpec((tm, tn), lambda i, j: (i, j))]
    out_shape = [jax.ShapeDtypeStruct((m, n), BF16)]
    if emit_kv:
        out_specs.append(pl.BlockSpec((tm, tn), lambda i, j: (i, jnp.clip(j - n_sec, 0, 2 * n_sec - 1))))
        out_shape.append(jax.ShapeDtypeStruct((m, 2 * d), F32))
    vmem = 2 * tm * d * 2 + 2 * d * tn * 2 + 2 * tm * tn * 2 + 4 * tm * tn * 4
    return pl.pallas_call(
        functools.partial(_attn_inproj_kernel, n_sec=n_sec, rope=rope, emit_kv=emit_kv),
        grid=(m // tm, n // tn),
        in_specs=in_specs,
        out_specs=out_specs,
        out_shape=out_shape,
        compiler_params=_cparams(("parallel", "arbitrary"), vmem),
        name="attn_inproj",
    )(*args)


def _rope_tables(n_tokens, head_dim):
    n_rows = n_tokens // GRID_W
    rows = jnp.repeat(jnp.arange(n_rows, dtype=F32), GRID_W)
    cols = jnp.tile(jnp.arange(GRID_W, dtype=F32), n_rows)
    axis_dim = head_dim // 2
    inv_freq = ROPE_THETA ** (-jnp.arange(0, axis_dim, 2, dtype=F32) / axis_dim)
    ar, ac = rows[:, None] * inv_freq, cols[:, None] * inv_freq
    zero = jnp.zeros_like(ar)
    cos = jnp.concatenate([jnp.cos(ar), jnp.cos(ar), jnp.cos(ac), jnp.cos(ac)], axis=1)
    s1 = jnp.concatenate([-jnp.sin(ar), zero, -jnp.sin(ac), zero], axis=1)
    s2 = jnp.concatenate([zero, jnp.sin(ar), zero, jnp.sin(ac)], axis=1)
    return cos, s1, s2


def _attn_kernel(*refs, lambda_init, head_dim, has_ctx):
    lq1, lk1, lq2, lk2, subln_ref, q_ref, k_ref, v_ref = refs[:8]
    pos = 8
    if has_ctx:
        kc_ref, vc_ref = refs[pos:pos + 2]
        pos += 2
    g_ref, o_ref = refs[pos], refs[pos + 1]
    lam = (jnp.exp(jnp.sum(lq1[...] * lk1[...], axis=-1, keepdims=True))
           - jnp.exp(jnp.sum(lq2[...] * lk2[...], axis=-1, keepdims=True)) + lambda_init)
    scale = head_dim ** -0.5
    nt = (((1,), (1,)), ((), ()))
    probs = []
    for t in range(2):
        cols = slice(t * head_dim, (t + 1) * head_dim)
        qt = q_ref[:, cols]
        s = lax.dot_general(qt, k_ref[:, cols], nt, preferred_element_type=F32)
        mx = jnp.max(s, axis=-1, keepdims=True)
        if has_ctx:
            sc = lax.dot_general(qt, kc_ref[:, cols].astype(BF16), nt, preferred_element_type=F32)
            mx = jnp.maximum(mx, jnp.max(sc, axis=-1, keepdims=True))
        e = jnp.exp((s - mx) * scale)
        den = jnp.sum(e, axis=-1, keepdims=True)
        if has_ctx:
            ec = jnp.exp((sc - mx) * scale)
            den = den + jnp.sum(ec, axis=-1, keepdims=True)
        inv = 1.0 / den
        probs.append((e * inv, ec * inv if has_ctx else None))
    w = (probs[0][0] - lam * probs[1][0]).astype(BF16)
    o = jnp.dot(w, v_ref[...], preferred_element_type=F32)
    if has_ctx:
        wc = (probs[0][1] - lam * probs[1][1]).astype(BF16)
        o = o + jnp.dot(wc, vc_ref[...].astype(BF16), preferred_element_type=F32)
    o = o * lax.rsqrt(jnp.mean(o * o, axis=-1, keepdims=True) + SUBLN_EPS) * subln_ref[...]
    o = o * (1.0 - lambda_init)
    o_ref[...] = (o * g_ref[...].astype(F32)).astype(o_ref.dtype)


def _attention(qkvg, lam_vecs, subln_w, lambda_init, n_batch, seq, n_heads, head_dim, ctx, tq):
    hw = 2 * head_dim
    d = n_heads * hw
    qt_per_seq = seq // tq
    vec = lambda b, h, i: (0, 0)
    in_specs = [pl.BlockSpec((1, head_dim), vec)] * 4 + [
        pl.BlockSpec((1, hw), vec),
        pl.BlockSpec((tq, hw), lambda b, h, i: (b * qt_per_seq + i, h)),
        pl.BlockSpec((seq, hw), lambda b, h, i: (b, n_heads + h)),
        pl.BlockSpec((seq, hw), lambda b, h, i: (b, 2 * n_heads + h)),
    ]
    args = [*lam_vecs, subln_w, qkvg, qkvg, qkvg]
    past = 0
    if ctx is not None:
        cache_k, cache_v, idx = ctx
        past = cache_k.shape[2]
        for c in (cache_k, cache_v):
            in_specs.append(pl.BlockSpec((None, None, past, hw), lambda b, h, i: (b, idx, 0, h)))
            args.append(c)
    in_specs.append(pl.BlockSpec((tq, hw), lambda b, h, i: (b * qt_per_seq + i, 3 * n_heads + h)))
    args.append(qkvg)
    vmem = 4 * seq * hw * 2 + 4 * past * hw * 4 + 8 * tq * hw * 2 + 6 * tq * (seq + past) * 4
    return pl.pallas_call(
        functools.partial(_attn_kernel, lambda_init=lambda_init, head_dim=head_dim, has_ctx=ctx is not None),
        grid=(n_batch, n_heads, qt_per_seq),
        in_specs=in_specs,
        out_specs=pl.BlockSpec((tq, hw), lambda b, h, i: (b * qt_per_seq + i, h)),
        out_shape=jax.ShapeDtypeStruct((n_batch * seq, d), BF16),
        compiler_params=_cparams(("parallel", "parallel", "parallel"), vmem),
        name="attn_core",
    )(*args)


def _outproj_kernel(a_ref, w_ref, x_ref, gate_ref, o_ref):
    acc = jnp.dot(a_ref[...], w_ref[...], preferred_element_type=F32)
    o_ref[...] = x_ref[...] + gate_ref[...] * acc


def _outproj(a, w, x, gate, layer, row_of_tile, tm, tn):
    m, k = a.shape
    d = w.shape[1]
    vmem = 2 * tm * k * 2 + 2 * k * tn * 2 + 6 * tm * tn * 4
    return pl.pallas_call(
        _outproj_kernel,
        grid=(m // tm, d // tn),
        in_specs=[pl.BlockSpec((tm, k), lambda i, j: (i, 0)),
                  pl.BlockSpec((k, tn), lambda i, j: (0, j)),
                  pl.BlockSpec((tm, tn), lambda i, j: (i, j)),
                  pl.BlockSpec((None, None, 1, tn), lambda i, j: (layer, row_of_tile(i), 0, j))],
        out_specs=pl.BlockSpec((tm, tn), lambda i, j: (i, j)),
        out_shape=jax.ShapeDtypeStruct((m, d), F32),
        compiler_params=_cparams(("parallel", "arbitrary"), vmem),
        name="outproj",
    )(a, w, x, gate)


def _ssm_inproj_kernel(h_ref, w_ref, o_ref, *, n_z):
    acc = jnp.dot(h_ref[...], w_ref[...], preferred_element_type=F32)

    @pl.when(pl.program_id(1) < n_z)
    def _():
        o_ref[...] = jax.nn.silu(acc).astype(o_ref.dtype)

    @pl.when(pl.program_id(1) >= n_z)
    def _():
        o_ref[...] = acc.astype(o_ref.dtype)


def _ssm_inproj(h, w, d_inner, tm, tn):
    m, d = h.shape
    n = w.shape[1]
    vmem = 2 * tm * d * 2 + 2 * d * tn * 2 + 2 * tm * tn * 2 + 4 * tm * tn * 4
    return pl.pallas_call(
        functools.partial(_ssm_inproj_kernel, n_z=d_inner // tn),
        grid=(m // tm, n // tn),
        in_specs=[pl.BlockSpec((tm, d), lambda i, j: (i, 0)), pl.BlockSpec((d, tn), lambda i, j: (0, j))],
        out_specs=pl.BlockSpec((tm, tn), lambda i, j: (i, j)),
        out_shape=jax.ShapeDtypeStruct((m, n), BF16),
        compiler_params=_cparams(("parallel", "arbitrary"), vmem),
        name="ssm_inproj",
    )(h, w)


def _dt_kernel(h_ref, w_ref, b_ref, o_ref):
    acc = jnp.dot(h_ref[...], w_ref[...], preferred_element_type=F32)
    o_ref[...] = jax.nn.softplus(acc + b_ref[...])


def _dt_proj(h, w, bias, tm):
    m, d = h.shape
    n = w.shape[1]
    return pl.pallas_call(
        _dt_kernel,
        grid=(m // tm,),
        in_specs=[pl.BlockSpec((tm, d), lambda i: (i, 0)), pl.BlockSpec((d, n), lambda i: (0, 0)),
                  pl.BlockSpec((1, n), lambda i: (0, 0))],
        out_specs=pl.BlockSpec((tm, n), lambda i: (i, 0)),
        out_shape=jax.ShapeDtypeStruct((m, n), F32),
        compiler_params=_cparams(("parallel",), 2 * tm * d * 2 + 2 * d * n * 2 + 6 * tm * n * 4),
        name="dt_proj",
    )(h, w, bias.reshape(1, n))


def _conv_kernel(x_ref, w_ref, b_ref, o_ref, pad_ref):
    seq = x_ref.shape[0]
    zeros = jnp.zeros((SUBLANES, x_ref.shape[1]), F32)
    pad_ref[0:SUBLANES, :] = zeros
    pad_ref[seq + SUBLANES:seq + 2 * SUBLANES, :] = zeros
    pad_ref[SUBLANES:seq + SUBLANES, :] = x_ref[...].astype(F32)
    acc = jnp.broadcast_to(b_ref[...], (seq, x_ref.shape[1]))
    for tap in range(D_CONV):
        acc = acc + w_ref[tap:tap + 1, :] * pad_ref[pl.ds(SUBLANES - D_CONV // 2 + tap, seq), :]
    o_ref[...] = jax.nn.silu(acc).astype(o_ref.dtype)


def _conv(zx, conv_w, conv_b, d_inner, n_batch, seq, tc):
    conv_dim = conv_w.shape[1]
    off = d_inner // tc
    vmem = 4 * seq * tc * 2 + (seq + 2 * SUBLANES) * tc * 4 + 4 * seq * tc * 4
    return pl.pallas_call(
        _conv_kernel,
        grid=(n_batch, conv_dim // tc),
        in_specs=[pl.BlockSpec((seq, tc), lambda b, j: (b, off + j)),
                  pl.BlockSpec((D_CONV, tc), lambda b, j: (0, j)),
                  pl.BlockSpec((1, tc), lambda b, j: (0, j))],
        out_specs=pl.BlockSpec((seq, tc), lambda b, j: (b, j)),
        out_shape=jax.ShapeDtypeStruct((n_batch * seq, conv_dim), BF16),
        scratch_shapes=[pltpu.VMEM((seq + 2 * SUBLANES, tc), F32)],
        compiler_params=_cparams(("parallel", "parallel"), vmem),
        name="ssm_conv",
    )(zx, conv_w, conv_b.reshape(1, conv_dim))


def _ssd_kernel(*refs, reverse, has_init, emit_final, add_tail, n_chunks, head_dim):
    x_ref, b_ref, c_ref, dt_ref, dtt_ref, alog_ref, alogt_ref = refs[:7]
    pos = 7
    if has_init:
        h0_ref = refs[pos]
        pos += 1
    if add_tail:
        d_ref, yprev_ref = refs[pos:pos + 2]
        pos += 2
    y_ref = refs[pos]
    pos += 1
    if emit_final:
        hout_ref = refs[pos]
        pos += 1
    ht_ref, xdec_ref = refs[pos], refs[pos + 1]

    t_len, gw = x_ref.shape
    hpg = gw // head_dim
    ci = pl.program_id(2)

    @pl.when(ci == 0)
    def _():
        if has_init:
            ht_ref[...] = h0_ref[...].T
        else:
            ht_ref[...] = jnp.zeros(ht_ref.shape, F32)

    a = dt_ref[...] * (-jnp.exp(alog_ref[...]))
    a_t = dtt_ref[...] * (-jnp.exp(alogt_ref[...]))
    row = lax.broadcasted_iota(jnp.int32, (t_len, t_len), 0)
    col = lax.broadcasted_iota(jnp.int32, (t_len, t_len), 1)
    tri = (row <= col) if reverse else (row >= col)
    trif = tri.astype(F32)
    hi = lax.Precision.HIGHEST
    cum = jnp.dot(trif, a, preferred_element_type=F32, precision=hi)
    cum_t = lax.dot_general(a_t, trif, (((1,), (1,)), ((), ())),
                            preferred_element_type=F32, precision=hi)
    last = 0 if reverse else t_len - 1
    tot = cum[last:last + 1, :]
    dec_in = jnp.exp(cum)
    dec_out = jnp.exp(tot - cum)
    dec_chunk = jnp.exp(tot)

    bm, cm = b_ref[...], c_ref[...]
    cb = lax.dot_general(cm, bm, (((1,), (1,)), ((), ())), preferred_element_type=F32)
    yoff = jnp.dot(cm, ht_ref[...].astype(BF16), preferred_element_type=F32)
    first = lax.broadcasted_iota(jnp.int32, (t_len, LANES), 1) < head_dim
    first_row = lax.broadcasted_iota(jnp.int32, (1, LANES), 1) < head_dim

    def pair(v, m, sel):
        return jnp.where(sel, v[:, 2 * m:2 * m + 1], v[:, 2 * m + 1:2 * m + 2])

    for m in range(hpg // 2):
        blk = slice(m * LANES, (m + 1) * LANES)
        xp = x_ref[:, blk].astype(F32)
        xdt = xp * pair(dt_ref[...], m, first)
        xdt_b = xdt.astype(BF16)
        ys = []
        for h in (2 * m, 2 * m + 1):
            seg = cum[:, h:h + 1] - cum_t[h:h + 1, :]
            decay = jnp.exp(jnp.where(tri, seg, -1e30))
            ys.append(jnp.dot((cb * decay).astype(BF16), xdt_b, preferred_element_type=F32))
        y = jnp.where(first, ys[0], ys[1]) + yoff[:, blk] * pair(dec_in, m, first)
        if add_tail:
            y = y + xp * pair(d_ref[...], m, first_row) + yprev_ref[:, blk].astype(F32)
        y_ref[:, blk] = y.astype(y_ref.dtype)
        xdec_ref[:, blk] = (xdt * pair(dec_out, m, first)).astype(BF16)

    bt = bm.astype(F32).T.astype(BF16)
    upd = jnp.dot(bt, xdec_ref[...], preferred_element_type=F32)
    for m in range(hpg // 2):
        blk = slice(m * LANES, (m + 1) * LANES)
        ht_ref[:, blk] = ht_ref[:, blk] * pair(dec_chunk, m, first_row) + upd[:, blk]

    if emit_final:
        @pl.when(ci == n_chunks - 1)
        def _():
            hout_ref[...] = ht_ref[...].T


def _ssd(xbc, dt_g, dtt_g, a_log, direction, n_batch, seq, d_inner, n_groups, d_state, head_dim,
         h0=None, tail=None, emit_final=False):
    t_len = min(SSD_CHUNK, seq)
    nc = seq // t_len
    gw = d_inner // n_groups
    hpg = gw // head_dim
    assert head_dim * 2 == LANES and hpg % 2 == 0 and d_state % LANES == 0
    reverse = direction == 1
    rows = (lambda b, c: b * nc + (nc - 1 - c)) if reverse else (lambda b, c: b * nc + c)
    bcol, ccol = d_inner // d_state, d_inner // d_state + n_groups
    in_specs = [
        pl.BlockSpec((t_len, gw), lambda b, g, c: (rows(b, c), g)),
        pl.BlockSpec((t_len, d_state), lambda b, g, c: (rows(b, c), bcol + g)),
        pl.BlockSpec((t_len, d_state), lambda b, g, c: (rows(b, c), ccol + g)),
        pl.BlockSpec((None, None, t_len, hpg), lambda b, g, c: (direction, g, rows(b, c), 0)),
        pl.BlockSpec((None, None, hpg, t_len), lambda b, g, c: (direction, g, 0, rows(b, c))),
        pl.BlockSpec((None, None, 1, hpg), lambda b, g, c: (direction, g, 0, 0)),
        pl.BlockSpec((None, None, hpg, 1), lambda b, g, c: (direction, g, 0, 0)),
    ]
    args = [xbc, xbc, xbc, dt_g, dtt_g, a_log.reshape(2, n_groups, 1, hpg), a_log.reshape(2, n_groups, hpg, 1)]
    if h0 is not None:
        state, layer = h0
        in_specs.append(pl.BlockSpec((None, None, None, None, gw, d_state),
                                     lambda b, g, c: (b, layer, direction, g, 0, 0)))
        args.append(state)
    if tail is not None:
        d_skip, y_prev = tail
        in_specs.append(pl.BlockSpec((None, 1, hpg), lambda b, g, c: (g, 0, 0)))
        in_specs.append(pl.BlockSpec((t_len, gw), lambda b, g, c: (rows(b, c), g)))
        args += [d_skip, y_prev]
    out_specs = [pl.BlockSpec((t_len, gw), lambda b, g, c: (rows(b, c), g))]
    out_shape = [jax.ShapeDtypeStruct((n_batch * seq, d_inner), BF16)]
    if emit_final:
        out_specs.append(pl.BlockSpec((None, None, gw, d_state), lambda b, g, c: (b, g, 0, 0)))
        out_shape.append(jax.ShapeDtypeStruct((n_batch, n_groups, gw, d_state), F32))
    vmem = (8 * t_len * gw * 2 + 4 * gw * d_state * 4 + d_state * gw * 4 + t_len * gw * 2
            + 6 * t_len * gw * 4 + 2 * d_state * gw * 4)
    return pl.pallas_call(
        functools.partial(_ssd_kernel, reverse=reverse, has_init=h0 is not None, emit_final=emit_final,
                          add_tail=tail is not None, n_chunks=nc, head_dim=head_dim),
        grid=(n_batch, n_groups, nc),
        in_specs=in_specs,
        out_specs=out_specs,
        out_shape=out_shape,
        scratch_shapes=[pltpu.VMEM((d_state, gw), F32), pltpu.VMEM((t_len, gw), BF16)],
        compiler_params=_cparams(("parallel", "parallel", "arbitrary"), vmem),
        name="ssd_scan",
    )(*args)


def _gate_norm_kernel(y_ref, z_ref, w_ref, o_ref):
    y = y_ref[...].astype(F32) * z_ref[...].astype(F32)
    o_ref[...] = (y * lax.rsqrt(jnp.mean(y * y, axis=-1, keepdims=True) + NORM_EPS) * w_ref[...]).astype(o_ref.dtype)


def _gate_norm(y, zx, norm_w, tm):
    m, di = y.shape
    return pl.pallas_call(
        _gate_norm_kernel,
        grid=(m // tm,),
        in_specs=[pl.BlockSpec((tm, di), lambda i: (i, 0)), pl.BlockSpec((tm, di), lambda i: (i, 0)),
                  pl.BlockSpec((1, di), lambda i: (0, 0))],
        out_specs=pl.BlockSpec((tm, di), lambda i: (i, 0)),
        out_shape=jax.ShapeDtypeStruct((m, di), BF16),
        compiler_params=_cparams(("parallel",), 6 * tm * di * 2 + 4 * tm * di * 4),
        name="gate_norm",
    )(y, zx, norm_w.reshape(1, di))


def kernel(x_prompt, x_sample, cache_attn_k, cache_attn_v, state_ssm, c, c_ctx, norm_w, w_ada, b_ada, w_attn_in, attn_lambda_q1, attn_lambda_k1, attn_lambda_q2, attn_lambda_k2, attn_subln_w, w_attn_out, w_ssm_in, ssm_conv_w, ssm_conv_b, ssm_dt_bias, ssm_A_log, ssm_D, ssm_norm_w, w_ssm_out, final_norm_w):
    n_ctx, seq, d = x_prompt.shape
    n_dec, dec_seq, _ = x_sample.shape
    depth = norm_w.shape[0]
    n_heads, head_dim = cache_attn_k.shape[3], cache_attn_k.shape[5]
    past = cache_attn_k.shape[2]
    ssm_heads, ssm_hd, d_state = state_ssm.shape[3:]
    d_inner = ssm_norm_w.shape[1]
    conv_dim = ssm_conv_w.shape[2]
    n_groups = (conv_dim - d_inner) // (2 * d_state)
    hpg = ssm_heads // n_groups
    assert n_dec + 1 <= MOD_ROWS and dec_seq % GRID_W == 0

    cvecs = jnp.zeros((MOD_ROWS, d), F32).at[0].set(c_ctx).at[1:1 + n_dec].set(c)
    mod = _adaln(cvecs, w_ada, b_ada).reshape(depth, MOD_ROWS, 3, d)
    gate = mod[:, :, 2:3, :]

    xp = x_prompt.reshape(n_ctx * seq, d)
    xs = x_sample.reshape(n_dec * dec_seq, d)
    tm_p = _tile(n_ctx * seq, 1024, seq)
    tm_s = _tile(dec_seq, 1024, SUBLANES)
    tm_norm_p, tm_norm_s = _tile(n_ctx * seq, 256, SUBLANES), _tile(dec_seq, 256, SUBLANES)
    row_p = lambda i: 0
    row_s = lambda tm: (lambda i: 1 + (i * tm) // dec_seq)
    rope_tabs = _rope_tables(dec_seq, head_dim)
    cache_k = cache_attn_k.reshape(n_dec, -1, past, d)
    cache_v = cache_attn_v.reshape(n_dec, -1, past, d)

    new_k, new_v, new_h = [], [], []
    for layer in range(depth):
        idx = layer // 2
        hp = _norm_mod(xp, norm_w, mod, layer, row_p, tm_norm_p)
        hs = _norm_mod(xs, norm_w, mod, layer, row_s(tm_norm_s), tm_norm_s)
        if layer % 2 == 0:
            lambda_init = 0.8 - 0.6 * math.exp(-0.3 * layer)
            w_in = w_attn_in[idx].astype(BF16)
            w_out = w_attn_out[idx].astype(BF16)
            lam_vecs = [v[idx].reshape(1, head_dim) for v in
                        (attn_lambda_q1, attn_lambda_k1, attn_lambda_q2, attn_lambda_k2)]
            subln = attn_subln_w[idx].reshape(1, 2 * head_dim)
            tn = _tile(d, 512)
            qkvg_p, kv_p = _attn_inproj(hp, w_in, None, True, tm_p, tn)
            qkvg_s, = _attn_inproj(hs, w_in, rope_tabs, False, tm_s, tn)
            new_k.append(kv_p[:, :d].reshape(n_ctx, seq, n_heads, 2, head_dim))
            new_v.append(kv_p[:, d:].reshape(n_ctx, seq, n_heads, 2 * head_dim))
            og_p = _attention(qkvg_p, lam_vecs, subln, lambda_init, n_ctx, seq, n_heads, head_dim, None,
                              _tile(seq, 256, SUBLANES))
            og_s = _attention(qkvg_s, lam_vecs, subln, lambda_init, n_dec, dec_seq, n_heads, head_dim,
                              (cache_k, cache_v, idx), _tile(dec_seq, 256, SUBLANES))
            xp = _outproj(og_p, w_out, xp, gate, layer, row_p, tm_p, tn)
            xs = _outproj(og_s, w_out, xs, gate, layer, row_s(tm_s), tm_s, tn)
        else:
            n_zx = d_inner + conv_dim
            w_zx = w_ssm_in[idx, :, :n_zx].astype(BF16)
            w_dt = w_ssm_in[idx, :, n_zx:].astype(BF16)
            w_out = w_ssm_out[idx].astype(BF16)
            a_log = ssm_A_log[idx]
            d_skip = ssm_D[idx].reshape(n_groups, 1, hpg)
            state = state_ssm.reshape(n_dec, -1, 2, n_groups, hpg * ssm_hd, d_state)
            tn = _tile(math.gcd(d_inner, conv_dim), 512)
            finals = []
            outs = []
            for h, x_res, nb, sl, tm, tmn, rowf, h0 in (
                    (hp, xp, n_ctx, seq, tm_p, tm_norm_p, lambda t: row_p, None),
                    (hs, xs, n_dec, dec_seq, tm_s, tm_norm_s, row_s, (state, idx))):
                m = nb * sl
                zx = _ssm_inproj(h, w_zx, d_inner, tm, tn)
                dt = _dt_proj(h, w_dt, ssm_dt_bias[idx].reshape(-1), tm)
                dt4 = dt.reshape(m, 2, n_groups, hpg)
                dt_g = jnp.transpose(dt4, (1, 2, 0, 3))
                dtt_g = jnp.transpose(dt4, (1, 2, 3, 0))
                xbc = _conv(zx, ssm_conv_w[idx], ssm_conv_b[idx], d_inner, nb, sl,
                            _tile(math.gcd(d_inner, conv_dim), max(LANES, (512 * 1024) // sl)))
                emit = h0 is None
                res_f = _ssd(xbc, dt_g, dtt_g, a_log, 0, nb, sl, d_inner, n_groups, d_state, ssm_hd,
                             h0=h0, emit_final=emit)
                res_b = _ssd(xbc, dt_g, dtt_g, a_log, 1, nb, sl, d_inner, n_groups, d_state, ssm_hd,
                             h0=h0, tail=(d_skip, res_f[0]), emit_final=emit)
                if emit:
                    finals = [res_f[1], res_b[1]]
                yn = _gate_norm(res_b[0], zx, ssm_norm_w[idx], tmn)
                tm_o = min(tm, 512)
                outs.append(_outproj(yn, w_out, x_res, gate, layer, rowf(tm_o), tm_o, _tile(d, 512)))
            xp, xs = outs
            new_h.append(jnp.stack([f.reshape(n_ctx, ssm_heads, ssm_hd, d_state) for f in finals], axis=1))
    y_prompt = _final_norm(xp, final_norm_w, tm_norm_p).reshape(n_ctx, seq, d)
    y_sample = _final_norm(xs, final_norm_w, tm_norm_s).reshape(n_dec, dec_seq, d)
    return (y_prompt, y_sample, jnp.stack(new_k, axis=1), jnp.stack(new_v, axis=1), jnp.stack(new_h, axis=1))
```
